```python
import math
import jax, jax.numpy as jnp
from jax import lax
import numpy as np

D_MODEL = 2048
BATCH = 2
SEQ = 8192
DEPTH = 4

N_A_LAYERS = DEPTH // 2
N_B_LAYERS = DEPTH - N_A_LAYERS
HEAD_DIM = 128
A_HEADS = D_MODEL // (2 * HEAD_DIM)
A_SUBHEADS = 2 * A_HEADS
A_V_DIM = 2 * HEAD_DIM
B_HEADS = D_MODEL // HEAD_DIM
D_FF = 5504
NUM_BUCKETS = 32
MAX_DISTANCE = 128
Q_BLOCK = 128
EPS = 1e-6
BIAS_INIT = 0.5

kernel_name = "yoco_diffattn_stickbreaking_macaron"


def _rms_norm(x, g):
    xf = x.astype(jnp.float32)
    y = xf * lax.rsqrt(jnp.mean(xf * xf, axis=-1, keepdims=True) + EPS)
    return (y * g.astype(jnp.float32)).astype(x.dtype)


def _swiglu(x, wi, wo):
    gate, up = jnp.split(x @ wi, 2, axis=-1)
    return (jax.nn.silu(gate) * up) @ wo


def _t5_bucket(qpos, kpos):
    n = jnp.maximum(qpos[:, None] - kpos[None, :], 0)
    max_exact = NUM_BUCKETS // 2
    nf = jnp.maximum(n, max_exact).astype(jnp.float32)
    large = max_exact + (jnp.log(nf / max_exact) / math.log(MAX_DISTANCE / max_exact)
                         * (NUM_BUCKETS - max_exact)).astype(jnp.int32)
    large = jnp.minimum(large, NUM_BUCKETS - 1)
    return jnp.where(n < max_exact, n, large)


def _to_blocks(t):
    return jnp.swapaxes(t.reshape(t.shape[0], t.shape[1] // Q_BLOCK, Q_BLOCK, *t.shape[2:]), 0, 1)


def _from_blocks(t):
    t = jnp.swapaxes(t, 0, 1)
    return t.reshape(t.shape[0], t.shape[1] * t.shape[2], *t.shape[3:])


def _diff_attention(h, wqkv, q_g, k_g, lam_vecs, subln_g, wo, rel_bias, lambda_init):
    B, S, _ = h.shape
    q, k, v = jnp.split(h @ wqkv, 3, axis=-1)
    q = _rms_norm(q.reshape(B, S, A_SUBHEADS, HEAD_DIM), q_g)
    k = _rms_norm(k.reshape(B, S, A_SUBHEADS, HEAD_DIM), k_g)
    v = v.reshape(B, S, A_HEADS, A_V_DIM)
    lf = lam_vecs.astype(jnp.float32)
    lam = jnp.exp(jnp.sum(lf[0] * lf[1])) - jnp.exp(jnp.sum(lf[2] * lf[3])) + lambda_init
    kpos = jnp.arange(S)
    scale = HEAD_DIM ** -0.5

    def block(args):
        qb, start = args
        qpos = start + jnp.arange(Q_BLOCK)
        logits = jnp.einsum('bqhd,bkhd->bhqk', qb, k).astype(jnp.float32) * scale
        bias = jnp.moveaxis(rel_bias[_t5_bucket(qpos, kpos)], -1, 0).astype(jnp.float32)
        causal = kpos[None, :] <= qpos[:, None]
        logits = jnp.where(causal, logits + bias[None], -jnp.inf)
        p = jax.nn.softmax(logits, axis=-1).reshape(B, A_HEADS, 2, Q_BLOCK, S)
        attn = p[:, :, 0] - lam * p[:, :, 1]
        return jnp.einsum('bhqk,bkhe->bqhe', attn.astype(v.dtype), v)

    starts = jnp.arange(S // Q_BLOCK) * Q_BLOCK
    o = _from_blocks(lax.map(block, (_to_blocks(q), starts)))
    o = _rms_norm(o, subln_g) * (1.0 - lambda_init)
    return o.reshape(B, S, D_MODEL) @ wo


def _stick_breaking(h, wq, k, v, wo):
    B, S, _ = h.shape
    q = (h @ wq).reshape(B, S, B_HEADS, HEAD_DIM)
    kpos = jnp.arange(S)
    scale = HEAD_DIM ** -0.5

    def block(args):
        qb, start = args
        qpos = start + jnp.arange(Q_BLOCK)
        z = jnp.einsum('bqhd,bkhd->bhqk', qb, k).astype(jnp.float32) * scale
        strict = kpos[None, :] < qpos[:, None]
        log_keep = jnp.where(strict, jax.nn.log_sigmoid(-z), 0.0)
        key_axis = log_keep.ndim - 1
        log_w = jax.nn.log_sigmoid(z) + lax.cumsum(log_keep, axis=key_axis, reverse=True) - log_keep
        w = jnp.where(strict, jnp.exp(log_w), 0.0)
        return jnp.einsum('bhqk,bkhd->bqhd', w.astype(v.dtype), v)

    starts = jnp.arange(S // Q_BLOCK) * Q_BLOCK
    o = _from_blocks(lax.map(block, (_to_blocks(q), starts)))
    return o.reshape(B, S, D_MODEL) @ wo


def setup_inputs(seed: int = 0) -> dict:
    key = jax.random.key(seed)
    ks = jax.random.split(key, 20)
    f32 = jnp.float32

    def w(k, shape, fan_in):
        return jax.random.normal(k, shape, f32) * fan_in ** -0.5

    def gain(k, shape):
        return 1.0 + 0.02 * jax.random.normal(k, shape, f32)

    return {
        "x": jax.random.normal(ks[0], (BATCH, SEQ, D_MODEL), f32),
        "ffn_pre_norm": gain(ks[1], (DEPTH, D_MODEL)),
        "ffn_pre_wi": w(ks[2], (DEPTH, D_MODEL, 2 * D_FF), D_MODEL),
        "ffn_pre_wo": w(ks[3], (DEPTH, D_FF, D_MODEL), D_FF),
        "mix_norm": gain(ks[4], (DEPTH, D_MODEL)),
        "ffn_post_norm": gain(ks[5], (DEPTH, D_MODEL)),
        "ffn_post_wi": w(ks[6], (DEPTH, D_MODEL, 2 * D_FF), D_MODEL),
        "ffn_post_wo": w(ks[7], (DEPTH, D_FF, D_MODEL), D_FF),
        "rel_bias": BIAS_INIT * jax.random.normal(ks[8], (NUM_BUCKETS, A_SUBHEADS), f32),
        "a_wqkv": w(ks[9], (N_A_LAYERS, D_MODEL, 3 * D_MODEL), D_MODEL),
        "a_q_norm": gain(ks[10], (N_A_LAYERS, HEAD_DIM)),
        "a_k_norm": gain(ks[11], (N_A_LAYERS, HEAD_DIM)),
        "a_lambda": 0.1 * jax.random.normal(ks[12], (N_A_LAYERS, 4, HEAD_DIM), f32),
        "a_subln": gain(ks[13], (N_A_LAYERS, A_V_DIM)),
        "a_wo": w(ks[14], (N_A_LAYERS, D_MODEL, D_MODEL), D_MODEL),
        "kv_norm": gain(ks[15], (D_MODEL,)),
        "b_wkv": w(ks[16], (D_MODEL, 2 * D_MODEL), D_MODEL),
        "b_wq": w(ks[17], (N_B_LAYERS, D_MODEL, D_MODEL), D_MODEL),
        "b_wo": w(ks[18], (N_B_LAYERS, D_MODEL, D_MODEL), D_MODEL),
    }


def reference(x, ffn_pre_norm, ffn_pre_wi, ffn_pre_wo, mix_norm, ffn_post_norm, ffn_post_wi,
              ffn_post_wo, rel_bias, a_wqkv, a_q_norm, a_k_norm, a_lambda, a_subln, a_wo,
              kv_norm, b_wkv, b_wq, b_wo):
    B, S, _ = x.shape
    k_sh = None
    v_sh = None
    for l in range(DEPTH):
        if l == N_A_LAYERS:
            k_sh, v_sh = jnp.split(_rms_norm(x, kv_norm) @ b_wkv, 2, axis=-1)
            k_sh = k_sh.reshape(B, S, B_HEADS, HEAD_DIM)
            v_sh = v_sh.reshape(B, S, B_HEADS, HEAD_DIM)
        x = x + 0.5 * _swiglu(_rms_norm(x, ffn_pre_norm[l]), ffn_pre_wi[l], ffn_pre_wo[l])
        h = _rms_norm(x, mix_norm[l])
        if l < N_A_LAYERS:
            lambda_init = 0.8 - 0.6 * math.exp(-0.3 * l)
            x = x + _diff_attention(h, a_wqkv[l], a_q_norm[l], a_k_norm[l], a_lambda[l],
                                    a_subln[l], a_wo[l], rel_bias, lambda_init)
        else:
            i = l - N_A_LAYERS
            x = x + _stick_breaking(h, b_wq[i], k_sh, v_sh, b_wo[i])
        x = x + 0.5 * _swiglu(_rms_norm(x, ffn_post_norm[l]), ffn_post_wi[l], ffn_post_wo[l])
    return x
```

```python
import functools
import math

import jax
import jax.numpy as jnp
from jax import lax
from jax.experimental import pallas as pl
from jax.experimental.pallas import tpu as pltpu

D_MODEL = 2048
DEPTH = 4
N_A_LAYERS = DEPTH // 2
HEAD_DIM = 128
A_HEADS = D_MODEL // (2 * HEAD_DIM)
A_SUBHEADS = 2 * A_HEADS
A_V_DIM = 2 * HEAD_DIM
B_HEADS = D_MODEL // HEAD_DIM
D_FF = 5504
NUM_BUCKETS = 32
MAX_DISTANCE = 128
EPS = 1e-6

LANES = 128
VMEM_LIMIT_CAP = 60 * 2**20
MASK_VALUE = -1e30

ROW_TILE = 512
FF_TILE = 512
D_FF_PAD = pl.cdiv(D_FF, FF_TILE) * FF_TILE
PROJ_COL_TILE = 1024
ATTN_TILE = 512
SB_Q_TILE = 512

F32 = jnp.float32
BF16 = jnp.bfloat16
_NT = (((1,), (1,)), ((), ()))


def _compiler_params(n_grid_dims, vmem_estimate_bytes):
    limit = min(int(vmem_estimate_bytes), VMEM_LIMIT_CAP)
    return pltpu.CompilerParams(
        dimension_semantics=("arbitrary",) * n_grid_dims, vmem_limit_bytes=limit)


def _rms_rows(x, gain):
    ms = jnp.mean(x * x, axis=-1, keepdims=True)
    return x * lax.rsqrt(ms + EPS) * gain


def _ffn_kernel(x_ref, g_ref, wi_ref, wo_ref, o_ref, xn_ref):
    c = pl.program_id(1)

    @pl.when(c == 0)
    def _():
        xn_ref[...] = _rms_rows(x_ref[...], g_ref[...]).astype(BF16)
        o_ref[...] = jnp.zeros_like(o_ref)

    gu = jnp.dot(xn_ref[...], wi_ref[0], preferred_element_type=F32)
    gate = gu[:, :FF_TILE]
    up = gu[:, FF_TILE:]
    h = (gate * jax.nn.sigmoid(gate) * up).astype(BF16)
    o_ref[...] += jnp.dot(h, wo_ref[...], preferred_element_type=F32)

    @pl.when(c == pl.num_programs(1) - 1)
    def _():
        o_ref[...] = x_ref[...] + 0.5 * o_ref[...]


def _ffn(x, gain, wi_packed, wo_padded):
    t, d = x.shape
    n_chunks = wi_packed.shape[0]
    vmem = (4 * ROW_TILE * d * 4
            + ROW_TILE * d * 2
            + 2 * 3 * d * FF_TILE * 2
            + 4 * ROW_TILE * FF_TILE * 4
            + ROW_TILE * d * 4)
    return pl.pallas_call(
        _ffn_kernel,
        grid=(t // ROW_TILE, n_chunks),
        in_specs=[
            pl.BlockSpec((ROW_TILE, d), lambda i, c: (i, 0)),
            pl.BlockSpec((1, d), lambda i, c: (0, 0)),
            pl.BlockSpec((1, d, 2 * FF_TILE), lambda i, c: (c, 0, 0)),
            pl.BlockSpec((FF_TILE, d), lambda i, c: (c, 0)),
        ],
        out_specs=pl.BlockSpec((ROW_TILE, d), lambda i, c: (i, 0)),
        out_shape=jax.ShapeDtypeStruct((t, d), F32),
        scratch_shapes=[pltpu.VMEM((ROW_TILE, d), BF16)],
        compiler_params=_compiler_params(2, vmem),
        name="ffn",
    )(x, gain.reshape(1, d), wi_packed, wo_padded)


def _pack_ffn_weights(wi, wo):
    n_layers, d, _ = wi.shape
    n_chunks = D_FF_PAD // FF_TILE
    pad = ((0, 0), (0, 0), (0, D_FF_PAD - D_FF))
    gate = jnp.pad(wi[..., :D_FF].astype(BF16), pad).reshape(n_layers, d, n_chunks, 1, FF_TILE)
    up = jnp.pad(wi[..., D_FF:].astype(BF16), pad).reshape(n_layers, d, n_chunks, 1, FF_TILE)
    packed = jnp.concatenate([gate, up], axis=3)
    packed = jnp.transpose(packed, (0, 2, 1, 3, 4)).reshape(n_layers, n_chunks, d, 2 * FF_TILE)
    wo_padded = jnp.pad(wo.astype(BF16), ((0, 0), (0, D_FF_PAD - D_FF), (0, 0)))
    return packed, wo_padded


def _norm_matmul_kernel(x_ref, g_ref, w_ref, cg_ref, o_ref, xn_ref, *, n_head_norm_tiles):
    j = pl.program_id(1)

    @pl.when(j == 0)
    def _():
        xn_ref[...] = _rms_rows(x_ref[...], g_ref[...]).astype(BF16)

    y = jnp.dot(xn_ref[...], w_ref[...], preferred_element_type=F32)

    @pl.when(j < n_head_norm_tiles)
    def _():
        for h in range(PROJ_COL_TILE // HEAD_DIM):
            cols = slice(h * HEAD_DIM, (h + 1) * HEAD_DIM)
            o_ref[:, cols] = _rms_rows(y[:, cols], cg_ref[:, cols]).astype(o_ref.dtype)

    @pl.when(j >= n_head_norm_tiles)
    def _():
        o_ref[...] = (y * cg_ref[...]).astype(o_ref.dtype)


def _norm_matmul(x, gain, w, col_gain, n_head_norm_cols):
    t, d = x.shape
    n = w.shape[1]
    vmem = (2 * ROW_TILE * d * 4 + ROW_TILE * d * 2 + 2 * d * PROJ_COL_TILE * 2
            + 2 * ROW_TILE * PROJ_COL_TILE * 2 + 3 * ROW_TILE * PROJ_COL_TILE * 4)
    kern = functools.partial(_norm_matmul_kernel,
                             n_head_norm_tiles=n_head_norm_cols // PROJ_COL_TILE)
    return pl.pallas_call(
        kern,
        grid=(t // ROW_TILE, n // PROJ_COL_TILE),
        in_specs=[
            pl.BlockSpec((ROW_TILE, d), lambda i, j: (i, 0)),
            pl.BlockSpec((1, d), lambda i, j: (0, 0)),
            pl.BlockSpec((d, PROJ_COL_TILE), lambda i, j: (0, j)),
            pl.BlockSpec((1, PROJ_COL_TILE), lambda i, j: (0, j)),
        ],
        out_specs=pl.BlockSpec((ROW_TILE, PROJ_COL_TILE), lambda i, j: (i, j)),
        out_shape=jax.ShapeDtypeStruct((t, n), BF16),
        scratch_shapes=[pltpu.VMEM((ROW_TILE, d), BF16)],
        compiler_params=_compiler_params(2, vmem),
        name="norm_matmul",
    )(x, gain.reshape(1, d), w, col_gain.reshape(1, n))


def _matmul_res_kernel(a_ref, w_ref, x_ref, o_ref):
    o_ref[...] = x_ref[...] + jnp.dot(a_ref[...], w_ref[...], preferred_element_type=F32)


def _matmul_res(a, w, x):
    t, d = x.shape
    k = a.shape[1]
    vmem = 2 * ROW_TILE * k * 2 + 2 * k * d * 2 + 5 * ROW_TILE * d * 4
    return pl.pallas_call(
        _matmul_res_kernel,
        grid=(t // ROW_TILE,),
        in_specs=[
            pl.BlockSpec((ROW_TILE, k), lambda i: (i, 0)),
            pl.BlockSpec((k, d), lambda i: (0, 0)),
            pl.BlockSpec((ROW_TILE, d), lambda i: (i, 0)),
        ],
        out_specs=pl.BlockSpec((ROW_TILE, d), lambda i: (i, 0)),
        out_shape=jax.ShapeDtypeStruct((t, d), F32),
        compiler_params=_compiler_params(1, vmem),
        name="matmul_res",
    )(a, w, x)


def _t5_bucket(n):
    max_exact = NUM_BUCKETS // 2
    nf = jnp.maximum(n, max_exact).astype(F32)
    large = max_exact + (jnp.log(nf / max_exact) / math.log(MAX_DISTANCE / max_exact)
                         * (NUM_BUCKETS - max_exact)).astype(jnp.int32)
    large = jnp.minimum(large, NUM_BUCKETS - 1)
    return jnp.where(n < max_exact, n, large)


def _bias_tiles(rel_bias):
    t = ATTN_TILE
    assert t >= MAX_DISTANCE
    dist = jnp.arange(t)[:, None] - jnp.arange(t)[None, :]
    rel = (rel_bias.astype(F32) - rel_bias[NUM_BUCKETS - 1].astype(F32)[None, :]).T
    diag = jnp.where(dist >= 0, rel[:, _t5_bucket(jnp.maximum(dist, 0))], MASK_VALUE)
    left = rel[:, _t5_bucket(dist + t)]
    return jnp.stack([diag, left], axis=1)


def _diff_attn_kernel(q_ref, k_ref, v_ref, bias_ref, lam_ref, sg_ref, o_ref,
                      m_ref, l_ref, acc_ref, *, lambda_init):
    t = ATTN_TILE
    i = pl.program_id(2)
    q = q_ref[0]

    def tile(j, bias_idx, first):
        start = pl.multiple_of(j * t, t)
        k = k_ref[0, pl.ds(start, t), :]
        v = v_ref[0, pl.ds(start, t), :]
        for mp in range(2):
            cols = slice(mp * HEAD_DIM, (mp + 1) * HEAD_DIM)
            s = lax.dot_general(q[:, cols], k[:, cols], _NT, preferred_element_type=F32)
            if bias_idx is not None:
                s = s + bias_ref[mp, bias_idx]
            s_max = jnp.max(s, axis=-1, keepdims=True)
            if first:
                m_new = s_max
                p = jnp.exp(s - m_new)
                l_ref[mp] = jnp.sum(p, axis=-1, keepdims=True)
                acc_ref[mp] = jnp.dot(p.astype(BF16), v, preferred_element_type=F32)
            else:
                m_old = m_ref[mp]
                m_new = jnp.maximum(m_old, s_max)
                alpha = jnp.exp(m_old - m_new)
                p = jnp.exp(s - m_new)
                l_ref[mp] = alpha * l_ref[mp] + jnp.sum(p, axis=-1, keepdims=True)
                acc_ref[mp] = alpha * acc_ref[mp] + jnp.dot(
                    p.astype(BF16), v, preferred_element_type=F32)
            m_ref[mp] = m_new

    tile(i, 0, True)

    @pl.when(i >= 1)
    def _():
        tile(i - 1, 1, False)

    def far(j, carry):
        tile(j, None, False)
        return carry

    lax.fori_loop(0, jnp.maximum(i - 1, 0), far, 0)

    lv = lam_ref[...]
    lam = (jnp.exp(jnp.sum(lv[0:1] * lv[1:2], axis=-1, keepdims=True))
           - jnp.exp(jnp.sum(lv[2:3] * lv[3:4], axis=-1, keepdims=True)) + lambda_init)
    o = acc_ref[0] / l_ref[0] - lam * (acc_ref[1] / l_ref[1])
    o_ref[0] = (_rms_rows(o, sg_ref[...]) * (1.0 - lambda_init)).astype(o_ref.dtype)


def _diff_attn(qkv, bias_tiles, lam_vecs, subln_g, lambda_init):
    b, s, _ = qkv.shape
    t = ATTN_TILE
    w = A_V_DIM
    vmem = (2 * 2 * s * w * 2
            + 2 * 4 * t * t * 4
            + 4 * t * w * 2 + 2 * t * w * 4 + 4 * t * LANES * 4
            + 6 * t * t * 4)
    kern = functools.partial(_diff_attn_kernel, lambda_init=lambda_init)
    return pl.pallas_call(
        kern,
        grid=(b, A_HEADS, s // t),
        in_specs=[
            pl.BlockSpec((1, t, w), lambda bi, h, i: (bi, i, h)),
            pl.BlockSpec((1, s, w), lambda bi, h, i: (bi, 0, A_HEADS + h)),
            pl.BlockSpec((1, s, w), lambda bi, h, i: (bi, 0, 2 * A_HEADS + h)),
            pl.BlockSpec((2, 2, t, t), lambda bi, h, i: (h, 0, 0, 0)),
            pl.BlockSpec((4, HEAD_DIM), lambda bi, h, i: (0, 0)),
            pl.BlockSpec((1, w), lambda bi, h, i: (0, 0)),
        ],
        out_specs=pl.BlockSpec((1, t, w), lambda bi, h, i: (bi, i, h)),
        out_shape=jax.ShapeDtypeStruct((b, s, D_MODEL), BF16),
        scratch_shapes=[pltpu.VMEM((2, t, 1), F32), pltpu.VMEM((2, t, 1), F32),
                        pltpu.VMEM((2, t, w), F32)],
        compiler_params=_compiler_params(3, vmem),
        name="diff_attn",
    )(qkv, qkv, qkv, bias_tiles, lam_vecs, subln_g.reshape(1, w))


def _sb_attn_kernel(q_ref, k_ref, v_ref, tri_ref, o_ref, r_ref, acc_ref):
    tq = SB_Q_TILE
    cw = HEAD_DIM
    chunks_per_tile = tq // cw
    i = pl.program_id(2)
    q = q_ref[0]
    tri = tri_ref[...]
    r_ref[...] = jnp.zeros_like(r_ref)
    acc_ref[...] = jnp.zeros_like(acc_ref)

    def chunk(start, diag_offset):
        k = k_ref[0, pl.ds(start, cw), :]
        v = v_ref[0, pl.ds(start, cw), :]
        z = lax.dot_general(q, k, _NT, preferred_element_type=F32)
        lp = jnp.log(1.0 + jnp.exp(-jnp.abs(z)))
        log_keep = jnp.minimum(-z, 0.0) - lp
        log_beta = jnp.minimum(z, 0.0) - lp
        if diag_offset is not None:
            row = lax.broadcasted_iota(jnp.int32, (tq, cw), 0)
            col = lax.broadcasted_iota(jnp.int32, (tq, cw), 1) + diag_offset
            strict = col < row
            log_keep = jnp.where(strict, log_keep, 0.0)
        hi = log_keep.astype(BF16)
        lo = (log_keep - hi.astype(F32)).astype(BF16)
        sums = (jnp.dot(hi, tri, preferred_element_type=F32)
                + jnp.dot(lo, tri, preferred_element_type=F32))
        w = jnp.exp(log_beta + sums[:, :cw] + r_ref[...])
        if diag_offset is not None:
            w = jnp.where(strict, w, 0.0)
        acc_ref[...] += jnp.dot(w.astype(BF16), v, preferred_element_type=F32)
        r_ref[...] += sums[:, cw:]

    for cc in reversed(range(chunks_per_tile)):
        chunk(pl.multiple_of(i * tq + cc * cw, cw), cc * cw)

    def far(it, carry):
        chunk(pl.multiple_of((chunks_per_tile * i - 1 - it) * cw, cw), None)
        return carry

    lax.fori_loop(0, chunks_per_tile * i, far, 0)
    o_ref[0] = acc_ref[...].astype(o_ref.dtype)


def _sb_attn(q, kv):
    b, s, _ = q.shape
    tq = SB_Q_TILE
    hd = HEAD_DIM
    jj = jnp.arange(hd)
    tri = jnp.concatenate([(jj[:, None] > jj[None, :]).astype(BF16),
                           jnp.ones((hd, hd), BF16)], axis=1)
    vmem = (2 * 2 * s * hd * 2 + 4 * tq * hd * 2 + 2 * hd * 2 * hd * 2
            + 2 * tq * hd * 4 + 16 * tq * hd * 4)
    return pl.pallas_call(
        _sb_attn_kernel,
        grid=(b, B_HEADS, s // tq),
        in_specs=[
            pl.BlockSpec((1, tq, hd), lambda bi, h, i: (bi, i, h)),
            pl.BlockSpec((1, s, hd), lambda bi, h, i: (bi, 0, h)),
            pl.BlockSpec((1, s, hd), lambda bi, h, i: (bi, 0, B_HEADS + h)),
            pl.BlockSpec((hd, 2 * hd), lambda bi, h, i: (0, 0)),
        ],
        out_specs=pl.BlockSpec((1, tq, hd), lambda bi, h, i: (bi, i, h)),
        out_shape=jax.ShapeDtypeStruct((b, s, D_MODEL), BF16),
        scratch_shapes=[pltpu.VMEM((tq, hd), F32), pltpu.VMEM((tq, hd), F32)],
        compiler_params=_compiler_params(3, vmem),
        name="sb_attn",
    )(q, kv, kv, tri)


def kernel(x, ffn_pre_norm, ffn_pre_wi, ffn_pre_wo, mix_norm, ffn_post_norm, ffn_post_wi,
           ffn_post_wo, rel_bias, a_wqkv, a_q_norm, a_k_norm, a_lambda, a_subln, a_wo,
           kv_norm, b_wkv, b_wq, b_wo):
    b, s, d = x.shape
    t = b * s
    scale = HEAD_DIM ** -0.5

    pre_wi, pre_wo = _pack_ffn_weights(ffn_pre_wi, ffn_pre_wo)
    post_wi, post_wo = _pack_ffn_weights(ffn_post_wi, ffn_post_wo)
    bias_tiles = _bias_tiles(rel_bias)
    ones_d = jnp.ones((d,), F32)

    xf = x.reshape(t, d)
    kv = None
    for l in range(DEPTH):
        if l == N_A_LAYERS:
            kv = _norm_matmul(xf, kv_norm, b_wkv.astype(BF16), jnp.ones((2 * d,), F32), 0)
            kv = kv.reshape(b, s, 2 * d)
        xf = _ffn(xf, ffn_pre_norm[l], pre_wi[l], pre_wo[l])
        if l < N_A_LAYERS:
            lambda_init = 0.8 - 0.6 * math.exp(-0.3 * l)
            col_gain = jnp.concatenate([jnp.tile(a_q_norm[l].astype(F32) * scale, A_SUBHEADS),
                                        jnp.tile(a_k_norm[l].astype(F32), A_SUBHEADS), ones_d])
            qkv = _norm_matmul(xf, mix_norm[l], a_wqkv[l].astype(BF16), col_gain, 2 * d)
            o = _diff_attn(qkv.reshape(b, s, 3 * d), bias_tiles, a_lambda[l].astype(F32),
                           a_subln[l].astype(F32), lambda_init)
            xf = _matmul_res(o.reshape(t, d), a_wo[l].astype(BF16), xf)
        else:
            i = l - N_A_LAYERS
            q = _norm_matmul(xf, mix_norm[l], b_wq[i].astype(BF16), ones_d * scale, 0)
            o = _sb_attn(q.reshape(b, s, d), kv)
            xf = _matmul_res(o.reshape(t, d), b_wo[i].astype(BF16), xf)
        xf = _ffn(xf, ffn_post_norm[l], post_wi[l], post_wo[l])
    return xf.reshape(b, s, d)
```

```python
import functools
import math

import jax
import jax.numpy as jnp
from jax import lax
from jax.experimental import pallas as pl
from jax.experimental.pallas import tpu as pltpu

D_MODEL = 2048
DEPTH = 4
N_A_LAYERS = DEPTH // 2
HEAD_DIM = 128
A_HEADS = D_MODEL // (2 * HEAD_DIM)
A_SUBHEADS = 2 * A_HEADS
A_V_DIM = 2 * HEAD_DIM
B_HEADS = D_MODEL // HEAD_DIM
D_FF = 5504
NUM_BUCKETS = 32
MAX_DISTANCE = 128
EPS = 1e-6

LANES = 128
VMEM_LIMIT_CAP = 60 * 2**20
MASK_VALUE = -1e30

ROW_TILE = 512
FF_TILE = 512
D_FF_PAD = pl.cdiv(D_FF, FF_TILE) * FF_TILE
PROJ_COL_TILE = 1024
ATTN_TILE = 512
SB_Q_TILE = 512
SB_KEY_BLOCK = 512
SB_CUMSUM_CHUNK = 256

F32 = jnp.float32
BF16 = jnp.bfloat16
_NT = (((1,), (1,)), ((), ()))


def _compiler_params(n_grid_dims, vmem_estimate_bytes):
    limit = min(int(vmem_estimate_bytes), VMEM_LIMIT_CAP)
    return pltpu.CompilerParams(
        dimension_semantics=("arbitrary",) * n_grid_dims, vmem_limit_bytes=limit)


def _rms_rows(x, gain):
    ms = jnp.mean(x * x, axis=-1, keepdims=True)
    return x * lax.rsqrt(ms + EPS) * gain


def _ffn_kernel(x_ref, g_ref, wg_ref, wu_ref, wo_ref, o_ref, xn_ref):
    c = pl.program_id(1)

    @pl.when(c == 0)
    def _():
        xn_ref[...] = _rms_rows(x_ref[...], g_ref[...]).astype(BF16)
        o_ref[...] = jnp.zeros_like(o_ref)

    xn = xn_ref[...]
    gate = jnp.dot(xn, wg_ref[...], preferred_element_type=F32)
    up = jnp.dot(xn, wu_ref[...], preferred_element_type=F32)
    h = (gate * jax.nn.sigmoid(gate) * up).astype(BF16)
    o_ref[...] += jnp.dot(h, wo_ref[...], preferred_element_type=F32)

    @pl.when(c == pl.num_programs(1) - 1)
    def _():
        o_ref[...] = x_ref[...] + 0.5 * o_ref[...]


def _ffn(x, gain, wi_padded, wo_padded, layer):
    t, d = x.shape
    n_chunks = D_FF_PAD // FF_TILE
    vmem = (4 * ROW_TILE * d * 4
            + ROW_TILE * d * 2
            + 2 * 3 * d * FF_TILE * 2
            + 4 * ROW_TILE * FF_TILE * 4
            + ROW_TILE * d * 4)
    return pl.pallas_call(
        _ffn_kernel,
        grid=(t // ROW_TILE, n_chunks),
        in_specs=[
            pl.BlockSpec((ROW_TILE, d), lambda i, c: (i, 0)),
            pl.BlockSpec((1, d), lambda i, c: (0, 0)),
            pl.BlockSpec((None, d, FF_TILE), lambda i, c: (layer, 0, c)),
            pl.BlockSpec((None, d, FF_TILE), lambda i, c: (layer, 0, n_chunks + c)),
            pl.BlockSpec((None, FF_TILE, d), lambda i, c: (layer, c, 0)),
        ],
        out_specs=pl.BlockSpec((ROW_TILE, d), lambda i, c: (i, 0)),
        out_shape=jax.ShapeDtypeStruct((t, d), F32),
        scratch_shapes=[pltpu.VMEM((ROW_TILE, d), BF16)],
        compiler_params=_compiler_params(2, vmem),
        name="ffn",
    )(x, gain.reshape(1, d), wi_padded, wi_padded, wo_padded)


def _pad_ffn_weights(wi, wo):
    pad = ((0, 0), (0, 0), (0, D_FF_PAD - D_FF))
    wi_padded = jnp.concatenate([jnp.pad(wi[..., :D_FF].astype(BF16), pad),
                                 jnp.pad(wi[..., D_FF:].astype(BF16), pad)], axis=-1)
    wo_padded = jnp.pad(wo.astype(BF16), ((0, 0), (0, D_FF_PAD - D_FF), (0, 0)))
    return wi_padded, wo_padded


def _norm_matmul_kernel(x_ref, g_ref, w_ref, cg_ref, o_ref, xn_ref, *, n_head_norm_tiles):
    j = pl.program_id(1)

    @pl.when(j == 0)
    def _():
        xn_ref[...] = _rms_rows(x_ref[...], g_ref[...]).astype(BF16)

    y = jnp.dot(xn_ref[...], w_ref[...], preferred_element_type=F32)

    @pl.when(j < n_head_norm_tiles)
    def _():
        for h in range(PROJ_COL_TILE // HEAD_DIM):
            cols = slice(h * HEAD_DIM, (h + 1) * HEAD_DIM)
            o_ref[:, cols] = _rms_rows(y[:, cols], cg_ref[:, cols]).astype(o_ref.dtype)

    @pl.when(j >= n_head_norm_tiles)
    def _():
        o_ref[...] = (y * cg_ref[...]).astype(o_ref.dtype)


def _norm_matmul(x, gain, w, col_gain, n_head_norm_cols):
    t, d = x.shape
    n = w.shape[1]
    vmem = (2 * ROW_TILE * d * 4 + ROW_TILE * d * 2 + 2 * d * PROJ_COL_TILE * 2
            + 2 * ROW_TILE * PROJ_COL_TILE * 2 + 3 * ROW_TILE * PROJ_COL_TILE * 4)
    kern = functools.partial(_norm_matmul_kernel,
                             n_head_norm_tiles=n_head_norm_cols // PROJ_COL_TILE)
    return pl.pallas_call(
        kern,
        grid=(t // ROW_TILE, n // PROJ_COL_TILE),
        in_specs=[
            pl.BlockSpec((ROW_TILE, d), lambda i, j: (i, 0)),
            pl.BlockSpec((1, d), lambda i, j: (0, 0)),
            pl.BlockSpec((d, PROJ_COL_TILE), lambda i, j: (0, j)),
            pl.BlockSpec((1, PROJ_COL_TILE), lambda i, j: (0, j)),
        ],
        out_specs=pl.BlockSpec((ROW_TILE, PROJ_COL_TILE), lambda i, j: (i, j)),
        out_shape=jax.ShapeDtypeStruct((t, n), BF16),
        scratch_shapes=[pltpu.VMEM((ROW_TILE, d), BF16)],
        compiler_params=_compiler_params(2, vmem),
        name="norm_matmul",
    )(x, gain.reshape(1, d), w, col_gain.reshape(1, n))


def _matmul_res_kernel(a_ref, w_ref, x_ref, o_ref):
    o_ref[...] = x_ref[...] + jnp.dot(a_ref[...], w_ref[...], preferred_element_type=F32)


def _matmul_res(a, w, x):
    t, d = x.shape
    k = a.shape[1]
    vmem = 2 * ROW_TILE * k * 2 + 2 * k * d * 2 + 5 * ROW_TILE * d * 4
    return pl.pallas_call(
        _matmul_res_kernel,
        grid=(t // ROW_TILE,),
        in_specs=[
            pl.BlockSpec((ROW_TILE, k), lambda i: (i, 0)),
            pl.BlockSpec((k, d), lambda i: (0, 0)),
            pl.BlockSpec((ROW_TILE, d), lambda i: (i, 0)),
        ],
        out_specs=pl.BlockSpec((ROW_TILE, d), lambda i: (i, 0)),
        out_shape=jax.ShapeDtypeStruct((t, d), F32),
        compiler_params=_compiler_params(1, vmem),
        name="matmul_res",
    )(a, w, x)


def _t5_bucket(n):
    max_exact = NUM_BUCKETS // 2
    nf = jnp.maximum(n, max_exact).astype(F32)
    large = max_exact + (jnp.log(nf / max_exact) / math.log(MAX_DISTANCE / max_exact)
                         * (NUM_BUCKETS - max_exact)).astype(jnp.int32)
    large = jnp.minimum(large, NUM_BUCKETS - 1)
    return jnp.where(n < max_exact, n, large)


def _bias_tiles(rel_bias):
    t = ATTN_TILE
    assert t >= MAX_DISTANCE
    dist = jnp.arange(t)[:, None] - jnp.arange(t)[None, :]
    rel = (rel_bias.astype(F32) - rel_bias[NUM_BUCKETS - 1].astype(F32)[None, :]).T

    def lookup(bucket):
        out = jnp.zeros((A_SUBHEADS,) + bucket.shape, F32)
        for bk in range(NUM_BUCKETS - 1):
            out = jnp.where(bucket[None] == bk, rel[:, bk][:, None, None], out)
        return out

    diag = jnp.where(dist >= 0, lookup(_t5_bucket(jnp.maximum(dist, 0))), MASK_VALUE)
    left = lookup(_t5_bucket(dist + t))
    return jnp.stack([diag, left], axis=1)


def _diff_attn_kernel(q_ref, k_ref, v_ref, bias_ref, lam_ref, sg_ref, o_ref,
                      m_ref, l_ref, acc_ref, *, lambda_init):
    t = ATTN_TILE
    i = pl.program_id(2)
    q = q_ref[0]

    def tile(j, bias_idx, first):
        start = pl.multiple_of(j * t, t)
        k = k_ref[0, pl.ds(start, t), :]
        v = v_ref[0, pl.ds(start, t), :]
        for mp in range(2):
            cols = slice(mp * HEAD_DIM, (mp + 1) * HEAD_DIM)
            s = lax.dot_general(q[:, cols], k[:, cols], _NT, preferred_element_type=F32)
            if bias_idx is not None:
                s = s + bias_ref[mp, bias_idx]
            s_max = jnp.max(s, axis=-1, keepdims=True)
            if first:
                m_new = s_max
                p = jnp.exp(s - m_new)
                l_ref[mp] = jnp.sum(p, axis=-1, keepdims=True)
                acc_ref[mp] = jnp.dot(p.astype(BF16), v, preferred_element_type=F32)
            else:
                m_old = m_ref[mp]
                m_new = jnp.maximum(m_old, s_max)
                alpha = jnp.exp(m_old - m_new)
                p = jnp.exp(s - m_new)
                l_ref[mp] = alpha * l_ref[mp] + jnp.sum(p, axis=-1, keepdims=True)
                acc_ref[mp] = alpha * acc_ref[mp] + jnp.dot(
                    p.astype(BF16), v, preferred_element_type=F32)
            m_ref[mp] = m_new

    tile(i, 0, True)

    @pl.when(i >= 1)
    def _():
        tile(i - 1, 1, False)

    def far(j, carry):
        tile(j, None, False)
        return carry

    lax.fori_loop(0, jnp.maximum(i - 1, 0), far, 0)

    lv = lam_ref[...]
    lam = (jnp.exp(jnp.sum(lv[0:1] * lv[1:2], axis=-1, keepdims=True))
           - jnp.exp(jnp.sum(lv[2:3] * lv[3:4], axis=-1, keepdims=True)) + lambda_init)
    o = acc_ref[0] / l_ref[0] - lam * (acc_ref[1] / l_ref[1])
    o_ref[0] = (_rms_rows(o, sg_ref[...]) * (1.0 - lambda_init)).astype(o_ref.dtype)


def _diff_attn(qkv, bias_tiles, lam_vecs, subln_g, lambda_init):
    b, s, _ = qkv.shape
    t = ATTN_TILE
    w = A_V_DIM
    vmem = (2 * 2 * s * w * 2
            + 2 * 4 * t * t * 4
            + 4 * t * w * 2 + 2 * t * w * 4 + 4 * t * LANES * 4
            + 6 * t * t * 4)
    kern = functools.partial(_diff_attn_kernel, lambda_init=lambda_init)
    return pl.pallas_call(
        kern,
        grid=(b, A_HEADS, s // t),
        in_specs=[
            pl.BlockSpec((1, t, w), lambda bi, h, i: (bi, i, h)),
            pl.BlockSpec((1, s, w), lambda bi, h, i: (bi, 0, A_HEADS + h)),
            pl.BlockSpec((1, s, w), lambda bi, h, i: (bi, 0, 2 * A_HEADS + h)),
            pl.BlockSpec((2, 2, t, t), lambda bi, h, i: (h, 0, 0, 0)),
            pl.BlockSpec((4, HEAD_DIM), lambda bi, h, i: (0, 0)),
            pl.BlockSpec((1, w), lambda bi, h, i: (0, 0)),
        ],
        out_specs=pl.BlockSpec((1, t, w), lambda bi, h, i: (bi, i, h)),
        out_shape=jax.ShapeDtypeStruct((b, s, D_MODEL), BF16),
        scratch_shapes=[pltpu.VMEM((2, t, 1), F32), pltpu.VMEM((2, t, 1), F32),
                        pltpu.VMEM((2, t, w), F32)],
        compiler_params=_compiler_params(3, vmem),
        name="diff_attn",
    )(qkv, qkv, qkv, bias_tiles, lam_vecs, subln_g.reshape(1, w))


def _sb_attn_kernel(q_ref, k_ref, v_ref, tri_ref, o_ref,
                    r_ref, acc_ref, lb_ref, hi_ref, lo_ref, tot_ref):
    tq = SB_Q_TILE
    kb = SB_KEY_BLOCK
    cw = SB_CUMSUM_CHUNK
    n_chunks = kb // cw
    i = pl.program_id(2)
    q = q_ref[0]
    tri = tri_ref[...]
    r_ref[...] = jnp.zeros_like(r_ref)
    acc_ref[...] = jnp.zeros_like(acc_ref)

    def logits(block_idx):
        k = k_ref[0, pl.ds(pl.multiple_of(block_idx * kb, kb), kb), :]
        return lax.dot_general(q, k, _NT, preferred_element_type=F32)

    def stage1_finish(z, slot, diagonal):
        mz = jnp.minimum(z, 0.0)
        lp = jnp.log2(1.0 + jnp.exp2(mz + mz - z))
        log_beta = mz - lp
        log_keep = log_beta - z
        if diagonal:
            strict = (lax.broadcasted_iota(jnp.int32, (tq, kb), 1)
                      < lax.broadcasted_iota(jnp.int32, (tq, kb), 0))
            log_keep = jnp.where(strict, log_keep, 0.0)
            log_beta = jnp.where(strict, log_beta, MASK_VALUE)
        lb_ref[slot] = log_beta
        hi = log_keep.astype(BF16)
        hi_ref[slot] = hi
        lo_ref[slot] = (log_keep - hi.astype(F32)).astype(BF16)
        for c in range(n_chunks):
            tot_ref[slot, c] = jnp.sum(log_keep[:, c * cw:(c + 1) * cw], axis=-1, keepdims=True)

    def suffix_sums(slot):
        return [jnp.dot(hi_ref[slot, :, c * cw:(c + 1) * cw], tri, preferred_element_type=F32)
                + jnp.dot(lo_ref[slot, :, c * cw:(c + 1) * cw], tri, preferred_element_type=F32)
                for c in range(n_chunks)]

    def stage2_finish(suffix, slot, block_idx):
        r = r_ref[...]
        acc = acc_ref[...]
        for c in reversed(range(n_chunks)):
            w = jnp.exp2(lb_ref[slot, :, c * cw:(c + 1) * cw] + suffix[c] + r)
            v = v_ref[0, pl.ds(pl.multiple_of(block_idx * kb + c * cw, cw), cw), :]
            acc = acc + jnp.dot(w.astype(BF16), v, preferred_element_type=F32)
            r = r + tot_ref[slot, c]
        r_ref[...] = r
        acc_ref[...] = acc

    stage1_finish(logits(i), 0, True)

    def step(it, carry):
        cur = it % 2
        z_next = logits(i - 1 - it)
        suffix = suffix_sums(cur)
        stage1_finish(z_next, 1 - cur, False)
        stage2_finish(suffix, cur, i - it)
        return carry

    lax.fori_loop(0, i, step, 0)
    last = i % 2
    stage2_finish(suffix_sums(last), last, 0)
    o_ref[0] = acc_ref[...].astype(o_ref.dtype)


def _sb_attn(q, kv):
    b, s, _ = q.shape
    tq = SB_Q_TILE
    kb = SB_KEY_BLOCK
    hd = HEAD_DIM
    cw = SB_CUMSUM_CHUNK
    assert tq == kb and kb % cw == 0
    jj = jnp.arange(cw)
    tri = (jj[:, None] > jj[None, :]).astype(BF16)
    vmem = (2 * 2 * s * hd * 2 + 4 * tq * hd * 2 + 2 * cw * cw * 2
            + 2 * tq * LANES * 4 + 2 * tq * kb * 8 + 2 * (kb // cw) * tq * LANES * 4
            + 8 * tq * kb * 4)
    return pl.pallas_call(
        _sb_attn_kernel,
        grid=(b, B_HEADS, s // tq),
        in_specs=[
            pl.BlockSpec((1, tq, hd), lambda bi, h, i: (bi, i, h)),
            pl.BlockSpec((1, s, hd), lambda bi, h, i: (bi, 0, h)),
            pl.BlockSpec((1, s, hd), lambda bi, h, i: (bi, 0, B_HEADS + h)),
            pl.BlockSpec((cw, cw), lambda bi, h, i: (0, 0)),
        ],
        out_specs=pl.BlockSpec((1, tq, hd), lambda bi, h, i: (bi, i, h)),
        out_shape=jax.ShapeDtypeStruct((b, s, D_MODEL), BF16),
        scratch_shapes=[pltpu.VMEM((tq, 1), F32), pltpu.VMEM((tq, hd), F32),
                        pltpu.VMEM((2, tq, kb), F32), pltpu.VMEM((2, tq, kb), BF16),
                        pltpu.VMEM((2, tq, kb), BF16), pltpu.VMEM((2, kb // cw, tq, 1), F32)],
        compiler_params=_compiler_params(3, vmem),
        name="sb_attn",
    )(q, kv, kv, tri)


def kernel(x, ffn_pre_norm, ffn_pre_wi, ffn_pre_wo, mix_norm, ffn_post_norm, ffn_post_wi,
           ffn_post_wo, rel_bias, a_wqkv, a_q_norm, a_k_norm, a_lambda, a_subln, a_wo,
           kv_norm, b_wkv, b_wq, b_wo):
    b, s, d = x.shape
    t = b * s
    scale = HEAD_DIM ** -0.5

    pre_wi, pre_wo = _pad_ffn_weights(ffn_pre_wi, ffn_pre_wo)
    post_wi, post_wo = _pad_ffn_weights(ffn_post_wi, ffn_post_wo)
    bias_tiles = _bias_tiles(rel_bias)
    ones_d = jnp.ones((d,), F32)

    xf = x.reshape(t, d)
    kv = None
    for l in range(DEPTH):
        if l == N_A_LAYERS:
            kv = _norm_matmul(xf, kv_norm, b_wkv.astype(BF16), jnp.ones((2 * d,), F32), 0)
            kv = kv.reshape(b, s, 2 * d)
        xf = _ffn(xf, ffn_pre_norm[l], pre_wi, pre_wo, l)
        if l < N_A_LAYERS:
            lambda_init = 0.8 - 0.6 * math.exp(-0.3 * l)
            col_gain = jnp.concatenate([jnp.tile(a_q_norm[l].astype(F32) * scale, A_SUBHEADS),
                                        jnp.tile(a_k_norm[l].astype(F32), A_SUBHEADS), ones_d])
            qkv = _norm_matmul(xf, mix_norm[l], a_wqkv[l].astype(BF16), col_gain, 2 * d)
            o = _diff_attn(qkv.reshape(b, s, 3 * d), bias_tiles, a_lambda[l].astype(F32),
                           a_subln[l].astype(F32), lambda_init)
            xf = _matmul_res(o.reshape(t, d), a_wo[l].astype(BF16), xf)
        else:
            i = l - N_A_LAYERS
            q = _norm_matmul(xf, mix_norm[l], b_wq[i].astype(BF16),
                             ones_d * (scale * math.log2(math.e)), 0)
            o = _sb_attn(q.reshape(b, s, d), kv)
            xf = _matmul_res(o.reshape(t, d), b_wo[i].astype(BF16), xf)
        xf = _ffn(xf, ffn_post_norm[l], post_wi, post_wo, l)
    return xf.reshape(b, s, d)
```
